```python
import jax, jax.numpy as jnp
from jax import lax
import numpy as np

D_MODEL = 1024
BATCH = 8
SEQ = 2048
DEPTH = 4
DEC_BATCH = 128
DEC_SEQ = 8
PAST_LEN = 16384
PAGE_SIZE = 128

N_MIXERS = 2
N_CONV_LAYERS = (DEPTH + 1) // 2
N_SGU_LAYERS = DEPTH // 2
D_CONV = D_MODEL
CONV_W = 3
D_SGU = 3 * D_MODEL
N_SGU_HEADS = 8
CHUNK = 128
D_FF = 2816
ALPHA = (2.0 * DEPTH) ** 0.25
BETA = (8.0 * DEPTH) ** -0.25
LN_EPS = 1e-5

kernel_name = 'hybrid_shortconv_chunkgmlp_macaron_deepnorm_step'


def layer_norm(x, g, b):
    xf = x.astype(jnp.float32)
    mu = jnp.mean(xf, axis=-1, keepdims=True)
    var = jnp.mean(jnp.square(xf - mu), axis=-1, keepdims=True)
    return ((xf - mu) * lax.rsqrt(var + LN_EPS) * g.astype(jnp.float32) + b.astype(jnp.float32)).astype(x.dtype)


def post_norm(x, sub, g, b):
    return layer_norm(ALPHA * x + sub, g, b)


def swiglu(x, w_in, w_out):
    gate, up = jnp.split(x @ w_in, 2, axis=-1)
    return (jax.nn.silu(gate) * up) @ w_out


def conv_mixer(x, buf, w_in, w_conv, w_out):
    b_gate, c_gate, h = jnp.split(x @ w_in, 3, axis=-1)
    g = c_gate * h
    gp = jnp.concatenate([buf.astype(g.dtype), g], axis=1)
    conv = lax.conv_general_dilated(
        gp, w_conv[:, None, :].astype(g.dtype), window_strides=(1,), padding='VALID',
        dimension_numbers=('NWC', 'WIO', 'NWC'), feature_group_count=D_CONV)
    y = (b_gate * conv) @ w_out
    return y, gp[:, -(CONV_W - 1):, :]


def chunk_mix(v, w_s, b_s):
    n, L, dv = v.shape
    lc = L if L < CHUNK else CHUNK
    n_chunks = -(-L // lc)
    pad = n_chunks * lc - L
    vp = jnp.pad(v, ((0, 0), (0, pad), (0, 0))) if pad else v
    vc = vp.reshape(n, n_chunks, lc, N_SGU_HEADS, dv // N_SGU_HEADS)
    ws = jnp.tril(w_s[:, :lc, :lc]).astype(v.dtype)
    bs = jnp.transpose(b_s[:, :lc]).astype(v.dtype)
    out = jnp.einsum('hts,bcshd->bcthd', ws, vc) + bs[:, :, None]
    return out.reshape(n, n_chunks * lc, dv)[:, :L, :]


def sgu_mixer(x, w_in, ln_g, ln_b, w_s, b_s, w_out):
    z = jax.nn.gelu(x @ w_in, approximate=False)
    u, v = jnp.split(z, 2, axis=-1)
    v = layer_norm(v, ln_g, ln_b)
    y = (u * chunk_mix(v, w_s, b_s)) @ w_out
    return y, v


def setup_inputs(seed: int = 0) -> dict:
    key = jax.random.key(seed)
    ks = jax.random.split(key, 18)

    def nrm(k, shape, scale):
        return jax.random.normal(k, shape, jnp.float32) * scale

    return {
        'x_prompt': nrm(ks[0], (BATCH, SEQ, D_MODEL), 1.0),
        'x_sample': nrm(ks[1], (DEC_BATCH, DEC_SEQ, D_MODEL), 1.0),
        'cache_conv': nrm(ks[2], (N_CONV_LAYERS, DEC_BATCH, CONV_W - 1, D_CONV), 1.0),
        'ln_g': 1.0 + nrm(ks[3], (DEPTH, 3, D_MODEL), 0.02),
        'ln_b': nrm(ks[4], (DEPTH, 3, D_MODEL), 0.02),
        'ffn1_w_in': nrm(ks[5], (DEPTH, D_MODEL, 2 * D_FF), D_MODEL ** -0.5),
        'ffn1_w_out': nrm(ks[6], (DEPTH, D_FF, D_MODEL), BETA * D_FF ** -0.5),
        'ffn2_w_in': nrm(ks[7], (DEPTH, D_MODEL, 2 * D_FF), D_MODEL ** -0.5),
        'ffn2_w_out': nrm(ks[8], (DEPTH, D_FF, D_MODEL), BETA * D_FF ** -0.5),
        'conv_w_in': nrm(ks[9], (N_CONV_LAYERS, D_MODEL, 3 * D_CONV), D_MODEL ** -0.5),
        'conv_w': nrm(ks[10], (N_CONV_LAYERS, CONV_W, D_CONV), CONV_W ** -0.5),
        'conv_w_out': nrm(ks[11], (N_CONV_LAYERS, D_CONV, D_MODEL), BETA * D_CONV ** -0.5),
        'sgu_w_in': nrm(ks[12], (N_SGU_LAYERS, D_MODEL, 2 * D_SGU), D_MODEL ** -0.5),
        'sgu_ln_g': 1.0 + nrm(ks[13], (N_SGU_LAYERS, D_SGU), 0.02),
        'sgu_ln_b': nrm(ks[14], (N_SGU_LAYERS, D_SGU), 0.02),
        'sgu_w_s': nrm(ks[15], (N_SGU_LAYERS, N_SGU_HEADS, CHUNK, CHUNK), CHUNK ** -0.5),
        'sgu_b_s': 1.0 + nrm(ks[16], (N_SGU_LAYERS, N_SGU_HEADS, CHUNK), 0.02),
        'sgu_w_out': nrm(ks[17], (N_SGU_LAYERS, D_SGU, D_MODEL), BETA * D_SGU ** -0.5),
    }


def reference(x_prompt, x_sample, cache_conv, ln_g, ln_b, ffn1_w_in, ffn1_w_out, ffn2_w_in, ffn2_w_out,
              conv_w_in, conv_w, conv_w_out, sgu_w_in, sgu_ln_g, sgu_ln_b, sgu_w_s, sgu_b_s, sgu_w_out):
    xp, xs = x_prompt, x_sample
    conv_p, conv_s, sgu_s = [], [], []
    for i in range(DEPTH):
        j = i // N_MIXERS
        xp = post_norm(xp, 0.5 * swiglu(xp, ffn1_w_in[i], ffn1_w_out[i]), ln_g[i, 0], ln_b[i, 0])
        xs = post_norm(xs, 0.5 * swiglu(xs, ffn1_w_in[i], ffn1_w_out[i]), ln_g[i, 0], ln_b[i, 0])
        if i % N_MIXERS == 0:
            zero_buf = jnp.zeros((xp.shape[0], CONV_W - 1, D_CONV), xp.dtype)
            yp, bp = conv_mixer(xp, zero_buf, conv_w_in[j], conv_w[j], conv_w_out[j])
            ys, bs = conv_mixer(xs, cache_conv[j], conv_w_in[j], conv_w[j], conv_w_out[j])
            conv_p.append(bp)
            conv_s.append(bs)
        else:
            yp, _ = sgu_mixer(xp, sgu_w_in[j], sgu_ln_g[j], sgu_ln_b[j], sgu_w_s[j], sgu_b_s[j], sgu_w_out[j])
            ys, vs = sgu_mixer(xs, sgu_w_in[j], sgu_ln_g[j], sgu_ln_b[j], sgu_w_s[j], sgu_b_s[j], sgu_w_out[j])
            sgu_s.append(vs)
        xp = post_norm(xp, yp, ln_g[i, 1], ln_b[i, 1])
        xs = post_norm(xs, ys, ln_g[i, 1], ln_b[i, 1])
        xp = post_norm(xp, 0.5 * swiglu(xp, ffn2_w_in[i], ffn2_w_out[i]), ln_g[i, 2], ln_b[i, 2])
        xs = post_norm(xs, 0.5 * swiglu(xs, ffn2_w_in[i], ffn2_w_out[i]), ln_g[i, 2], ln_b[i, 2])
    return (xp, xs, jnp.stack(conv_p), jnp.stack(conv_s), jnp.stack(sgu_s))
```

```python
import functools

import jax
import jax.numpy as jnp
from jax import lax
from jax.experimental import pallas as pl
from jax.experimental.pallas import tpu as pltpu

F32 = jnp.float32
BF16 = jnp.bfloat16

LN_EPS = 1e-5
CHUNK = 128
CONV_W = 3
SUBLANES = 8
ROW_TILE = 512
VMEM_LIMIT_BYTES = 56 * 1024 * 1024


def _layer_norm(y, g, b):
    mu = jnp.mean(y, axis=-1, keepdims=True)
    d = y - mu
    var = jnp.mean(d * d, axis=-1, keepdims=True)
    return d * lax.rsqrt(var + LN_EPS) * g + b


def _dot(a, b):
    return jnp.dot(a, b, preferred_element_type=F32)


def _gelu(x):
    return 0.5 * x * (1.0 + lax.erf(x * 0.7071067811865476))


def _ffn_body(x_ref, win_ref, wout_ref, g_ref, b_ref, o_ref, *, alpha, d_ff):
    x = x_ref[...]
    xb = x.astype(BF16)
    gate = _dot(xb, win_ref[:, :d_ff])
    up = _dot(xb, win_ref[:, d_ff:])
    a = (gate * jax.nn.sigmoid(gate) * up).astype(BF16)
    y = _dot(a, wout_ref[...])
    o_ref[...] = _layer_norm(alpha * x + 0.5 * y, g_ref[...], b_ref[...])


def _resident(shape, layer):
    nd = len(shape)
    return pl.BlockSpec((None,) + tuple(shape), lambda *_: (layer,) + (0,) * nd,
                        pipeline_mode=pl.Buffered(1))


def _params(n_axes):
    return pltpu.CompilerParams(dimension_semantics=("arbitrary",) * n_axes,
                                vmem_limit_bytes=VMEM_LIMIT_BYTES)


def _ffn(x, w_in, w_out, ln_g, ln_b, layer, sub, alpha):
    m, d = x.shape
    d_ff = w_out.shape[1]
    tm = min(ROW_TILE, m)
    vec = pl.BlockSpec((None, None, 1, d), lambda i: (layer, sub, 0, 0))
    return pl.pallas_call(
        functools.partial(_ffn_body, alpha=alpha, d_ff=d_ff),
        grid=(m // tm,),
        in_specs=[pl.BlockSpec((tm, d), lambda i: (i, 0)),
                  _resident(w_in.shape[1:], layer),
                  _resident(w_out.shape[1:], layer),
                  vec, vec],
        out_specs=pl.BlockSpec((tm, d), lambda i: (i, 0)),
        out_shape=jax.ShapeDtypeStruct((m, d), F32),
        compiler_params=_params(1),
        name=f"ffn_l{layer}_s{sub}_m{m}",
    )(x, w_in, w_out, ln_g, ln_b)


def _conv_taps(gbuf_ref, g, tm):
    gbuf_ref[pl.ds(SUBLANES, tm), :] = g
    g1 = gbuf_ref[pl.ds(SUBLANES - 1, tm), :]
    g2 = gbuf_ref[pl.ds(SUBLANES - 2, tm), :]
    return g1, g2


def _conv_tail(x, b_gate, g, g1, g2, cw_ref, wout_ref, g_ref, b_ref, alpha):
    cw = cw_ref[...]
    conv = cw[0:1, :] * g2 + cw[1:2, :] * g1 + cw[2:3, :] * g
    y = _dot((b_gate * conv).astype(BF16), wout_ref[...])
    return _layer_norm(alpha * x + y, g_ref[...], b_ref[...])


def _conv_prompt_body(x_ref, win_ref, cw_ref, wout_ref, g_ref, b_ref, o_ref, tail_ref, gbuf_ref,
                      *, alpha, d_conv, tm):
    @pl.when(pl.program_id(1) == 0)
    def _():
        gbuf_ref[pl.ds(0, SUBLANES), :] = jnp.zeros((SUBLANES, d_conv), F32)

    x = x_ref[...]
    xb = x.astype(BF16)
    b_gate = _dot(xb, win_ref[:, :d_conv])
    g = _dot(xb, win_ref[:, d_conv:2 * d_conv]) * _dot(xb, win_ref[:, 2 * d_conv:])
    g1, g2 = _conv_taps(gbuf_ref, g, tm)
    o_ref[...] = _conv_tail(x, b_gate, g, g1, g2, cw_ref, wout_ref, g_ref, b_ref, alpha)
    last = gbuf_ref[pl.ds(tm, SUBLANES), :]
    gbuf_ref[pl.ds(0, SUBLANES), :] = last
    tail_ref[...] = last


def _conv_sample_body(x_ref, p1_ref, p2_ref, win_ref, cw_ref, wout_ref, g_ref, b_ref, o_ref, gout_ref,
                      gbuf_ref, *, alpha, d_conv, tm, seq):
    gbuf_ref[pl.ds(0, SUBLANES), :] = jnp.zeros((SUBLANES, d_conv), F32)
    x = x_ref[...]
    xb = x.astype(BF16)
    b_gate = _dot(xb, win_ref[:, :d_conv])
    g = _dot(xb, win_ref[:, d_conv:2 * d_conv]) * _dot(xb, win_ref[:, 2 * d_conv:])
    g1, g2 = _conv_taps(gbuf_ref, g, tm)
    pos = lax.broadcasted_iota(jnp.int32, (tm, d_conv), 0) % seq
    g1 = jnp.where(pos < 1, p1_ref[...], g1)
    g2 = jnp.where(pos < 2, p2_ref[...], g2)
    o_ref[...] = _conv_tail(x, b_gate, g, g1, g2, cw_ref, wout_ref, g_ref, b_ref, alpha)
    gout_ref[...] = g


def _conv_prompt(x, n_seq, w_in, cw, w_out, ln_g, ln_b, layer, j, alpha):
    m, d = x.shape
    d_conv = w_out.shape[1]
    seq = m // n_seq
    tm = min(ROW_TILE, seq)
    tiles = seq // tm
    vec = pl.BlockSpec((None, None, 1, d), lambda s, t: (layer, 1, 0, 0))
    out, tail = pl.pallas_call(
        functools.partial(_conv_prompt_body, alpha=alpha, d_conv=d_conv, tm=tm),
        grid=(n_seq, tiles),
        in_specs=[pl.BlockSpec((tm, d), lambda s, t: (s * tiles + t, 0)),
                  _resident(w_in.shape[1:], j),
                  _resident(cw.shape[1:], j),
                  _resident(w_out.shape[1:], j),
                  vec, vec],
        out_specs=[pl.BlockSpec((tm, d), lambda s, t: (s * tiles + t, 0)),
                   pl.BlockSpec((None, SUBLANES, d_conv), lambda s, t: (s, 0, 0))],
        out_shape=[jax.ShapeDtypeStruct((m, d), F32),
                   jax.ShapeDtypeStruct((n_seq, SUBLANES, d_conv), F32)],
        scratch_shapes=[pltpu.VMEM((tm + SUBLANES, d_conv), F32)],
        compiler_params=_params(2),
        name=f"conv_prompt_l{layer}",
    )(x, w_in, cw, w_out, ln_g, ln_b)
    return out, tail[:, SUBLANES - (CONV_W - 1):, :]


def _conv_sample(x, cache, w_in, cw, w_out, ln_g, ln_b, layer, j, alpha):
    m, d = x.shape
    d_conv = w_out.shape[1]
    n_seq = cache.shape[0]
    seq = m // n_seq
    assert seq == SUBLANES and CONV_W - 1 <= seq
    tm = min(ROW_TILE, m)
    pad = ((0, 0), (0, seq - (CONV_W - 1)), (0, 0))
    p2 = jnp.pad(cache, pad).reshape(m, d_conv)
    p1 = jnp.pad(cache[:, 1:, :], ((0, 0), (0, seq - 1), (0, 0))).reshape(m, d_conv)
    vec = pl.BlockSpec((None, None, 1, d), lambda i: (layer, 1, 0, 0))
    row = lambda i: (i, 0)
    out, g = pl.pallas_call(
        functools.partial(_conv_sample_body, alpha=alpha, d_conv=d_conv, tm=tm, seq=seq),
        grid=(m // tm,),
        in_specs=[pl.BlockSpec((tm, d), row),
                  pl.BlockSpec((tm, d_conv), row),
                  pl.BlockSpec((tm, d_conv), row),
                  _resident(w_in.shape[1:], j),
                  _resident(cw.shape[1:], j),
                  _resident(w_out.shape[1:], j),
                  vec, vec],
        out_specs=[pl.BlockSpec((tm, d), row), pl.BlockSpec((tm, d_conv), row)],
        out_shape=[jax.ShapeDtypeStruct((m, d), F32), jax.ShapeDtypeStruct((m, d_conv), F32)],
        scratch_shapes=[pltpu.VMEM((tm + SUBLANES, d_conv), F32)],
        compiler_params=_params(1),
        name=f"conv_sample_l{layer}",
    )(x, p1, p2, w_in, cw, w_out, ln_g, ln_b)
    return out, g.reshape(n_seq, seq, d_conv)[:, seq - (CONV_W - 1):, :]


def _sgu_body(x_ref, win_ref, vg_ref, vb_ref, ws_ref, bs_ref, wout_ref, g_ref, b_ref, o_ref, *rest,
              alpha, d_sgu, n_heads, tm, period, emit_v):
    if emit_v:
        v_ref, gated_ref = rest
    else:
        (gated_ref,) = rest
    x = x_ref[...]
    xb = x.astype(BF16)
    u = _gelu(_dot(xb, win_ref[:, :d_sgu]))
    v = _layer_norm(_gelu(_dot(xb, win_ref[:, d_sgu:])), vg_ref[...], vb_ref[...])
    if emit_v:
        v_ref[...] = v
    vb = v.astype(BF16)
    t_idx = lax.broadcasted_iota(jnp.int32, (CHUNK, CHUNK), 0)
    s_idx = lax.broadcasted_iota(jnp.int32, (CHUNK, CHUNK), 1)
    mask = s_idx <= t_idx
    if period != CHUNK:
        mask = mask & ((t_idx // period) == (s_idx // period))
    dh = d_sgu // n_heads
    for h in range(n_heads):
        w = jnp.where(mask, ws_ref[h], 0.0).astype(BF16)
        bias = bs_ref[:, h:h + 1]
        for c in range(tm // CHUNK):
            rows = slice(c * CHUNK, (c + 1) * CHUNK)
            cols = slice(h * dh, (h + 1) * dh)
            mix = _dot(w, vb[rows, cols]) + bias
            gated_ref[rows, cols] = (u[rows, cols] * mix).astype(BF16)
    y = _dot(gated_ref[...], wout_ref[...])
    o_ref[...] = _layer_norm(alpha * x + y, g_ref[...], b_ref[...])


def _sgu(x, w_in, v_g, v_b, w_s, b_s, w_out, ln_g, ln_b, layer, j, alpha, period, emit_v):
    m, d = x.shape
    d_sgu = w_out.shape[1]
    n_heads = w_s.shape[0]
    tm = min(ROW_TILE, m)
    vec = pl.BlockSpec((None, None, 1, d), lambda i: (layer, 1, 0, 0))
    svec = pl.BlockSpec((None, 1, d_sgu), lambda i: (j, 0, 0))
    row = lambda i: (i, 0)
    out_specs = [pl.BlockSpec((tm, d), row)]
    out_shape = [jax.ShapeDtypeStruct((m, d), F32)]
    if emit_v:
        out_specs.append(pl.BlockSpec((tm, d_sgu), row))
        out_shape.append(jax.ShapeDtypeStruct((m, d_sgu), F32))
    res = pl.pallas_call(
        functools.partial(_sgu_body, alpha=alpha, d_sgu=d_sgu, n_heads=n_heads, tm=tm, period=period,
                          emit_v=emit_v),
        grid=(m // tm,),
        in_specs=[pl.BlockSpec((tm, d), row),
                  _resident(w_in.shape[1:], j),
                  svec, svec,
                  pl.BlockSpec(w_s.shape, lambda i: (0, 0, 0), pipeline_mode=pl.Buffered(1)),
                  pl.BlockSpec(b_s.shape, lambda i: (0, 0), pipeline_mode=pl.Buffered(1)),
                  _resident(w_out.shape[1:], j),
                  vec, vec],
        out_specs=out_specs,
        out_shape=out_shape,
        scratch_shapes=[pltpu.VMEM((tm, d_sgu), BF16)],
        compiler_params=_params(1),
        name=f"sgu_l{layer}_m{m}",
    )(x, w_in, v_g, v_b, w_s, b_s, w_out, ln_g, ln_b)
    return res if emit_v else (res[0], None)


def kernel(x_prompt, x_sample, cache_conv, ln_g, ln_b, ffn1_w_in, ffn1_w_out, ffn2_w_in, ffn2_w_out,
           conv_w_in, conv_w, conv_w_out, sgu_w_in, sgu_ln_g, sgu_ln_b, sgu_w_s, sgu_b_s, sgu_w_out):
    batch, seq, d = x_prompt.shape
    dec_batch, dec_seq, _ = x_sample.shape
    depth = ln_g.shape[0]
    alpha = (2.0 * depth) ** 0.25
    n_heads = sgu_w_s.shape[1]
    d_sgu = sgu_w_out.shape[1]
    assert seq % CHUNK == 0 and CHUNK % dec_seq == 0 and dec_seq <= CHUNK

    xp = x_prompt.reshape(batch * seq, d)
    xs = x_sample.reshape(dec_batch * dec_seq, d)
    ln_g4 = ln_g.reshape(depth, 3, 1, d)
    ln_b4 = ln_b.reshape(depth, 3, 1, d)
    f1_in, f1_out = ffn1_w_in.astype(BF16), ffn1_w_out.astype(BF16)
    f2_in, f2_out = ffn2_w_in.astype(BF16), ffn2_w_out.astype(BF16)
    c_in, c_out = conv_w_in.astype(BF16), conv_w_out.astype(BF16)
    s_in, s_out = sgu_w_in.astype(BF16), sgu_w_out.astype(BF16)
    s_g = sgu_ln_g.reshape(-1, 1, d_sgu)
    s_b = sgu_ln_b.reshape(-1, 1, d_sgu)

    conv_p, conv_s, sgu_s = [], [], []
    for i in range(depth):
        j = i // 2
        xp = _ffn(xp, f1_in, f1_out, ln_g4, ln_b4, i, 0, alpha)
        xs = _ffn(xs, f1_in, f1_out, ln_g4, ln_b4, i, 0, alpha)
        if i % 2 == 0:
            xp, bp = _conv_prompt(xp, batch, c_in, conv_w, c_out, ln_g4, ln_b4, i, j, alpha)
            xs, bs = _conv_sample(xs, cache_conv[j], c_in, conv_w, c_out, ln_g4, ln_b4, i, j, alpha)
            conv_p.append(bp)
            conv_s.append(bs)
        else:
            rep = CHUNK // dec_seq
            ws_p = sgu_w_s[j]
            bs_p = jnp.transpose(sgu_b_s[j])
            ws_s = jnp.tile(sgu_w_s[j][:, :dec_seq, :dec_seq], (1, rep, rep))
            bs_s = jnp.tile(jnp.transpose(sgu_b_s[j][:, :dec_seq]), (rep, 1))
            xp, _ = _sgu(xp, s_in, s_g, s_b, ws_p, bs_p, s_out, ln_g4, ln_b4, i, j, alpha, CHUNK, False)
            xs, vs = _sgu(xs, s_in, s_g, s_b, ws_s, bs_s, s_out, ln_g4, ln_b4, i, j, alpha, dec_seq, True)
            sgu_s.append(vs.reshape(dec_batch, dec_seq, d_sgu))
        xp = _ffn(xp, f2_in, f2_out, ln_g4, ln_b4, i, 2, alpha)
        xs = _ffn(xs, f2_in, f2_out, ln_g4, ln_b4, i, 2, alpha)
    return (xp.reshape(batch, seq, d), xs.reshape(dec_batch, dec_seq, d),
            jnp.stack(conv_p), jnp.stack(conv_s), jnp.stack(sgu_s))
```
